```python
import math
import jax
import jax.numpy as jnp
from jax import lax
import numpy as np

D_MODEL = 2048
BATCH = 4
SEQ = 2048
DEPTH = 4

GRID_W = 64
CTX_LEN = 256
D_HEAD = 128
ATTN_Q_HEADS = 8
ATTN_KV_HEADS = 2
ATTN_GROUP = ATTN_Q_HEADS // ATTN_KV_HEADS
WINDOW = 128
ATTN_BLOCK = 128
GDN_HEADS = 8
GDN_DK = 128
GDN_DV = 128
GDN_CHUNK = 64
CONV_W = 5
D_FF = 4 * D_MODEL
ROPE_BASE = 10000.0
AXIS_ROT = D_HEAD // 2
EPS = 1e-6
NEG_INF = -1e30

ATTN_Q = ATTN_Q_HEADS * D_HEAD
ATTN_KV = ATTN_KV_HEADS * D_HEAD
GDN_QK = GDN_HEADS * GDN_DK
GDN_V = GDN_HEADS * GDN_DV
GDN_QKV = 2 * GDN_QK + GDN_V
GDN_AB = 4 * GDN_HEADS
IN_OFFSETS = (ATTN_Q, ATTN_Q + ATTN_KV, ATTN_Q + 2 * ATTN_KV, ATTN_Q + 2 * ATTN_KV + GDN_QKV,
              ATTN_Q + 2 * ATTN_KV + GDN_QKV + GDN_V)
D_IN = ATTN_Q + 2 * ATTN_KV + GDN_QKV + GDN_V + GDN_AB
MIX_OUT = ATTN_Q + GDN_V

kernel_name = "hymba_swa_gdn_sqrelu_prefix_dit"


def rmsnorm(x, gain):
    xf = x.astype(jnp.float32)
    y = xf * lax.rsqrt(jnp.mean(xf * xf, axis=-1, keepdims=True) + EPS)
    return (y * gain.astype(jnp.float32)).astype(x.dtype)


def modulate(h, shift, scale):
    return h * (1 + scale) + shift


def axial_rope_tables(n_tokens, dtype):
    rows = n_tokens // GRID_W
    row = jnp.repeat(jnp.arange(rows, dtype=jnp.float32), GRID_W)
    col = jnp.tile(jnp.arange(GRID_W, dtype=jnp.float32), rows)
    inv_freq = ROPE_BASE ** (-jnp.arange(0, AXIS_ROT, 2, dtype=jnp.float32) / AXIS_ROT)
    ang = jnp.concatenate([row[:, None] * inv_freq, col[:, None] * inv_freq], axis=-1)
    return jnp.cos(ang).astype(dtype), jnp.sin(ang).astype(dtype)


def axial_rope(x, cos, sin):
    half = AXIS_ROT // 2

    def rot(t, c, s):
        t1, t2 = t[..., :half], t[..., half:]
        return jnp.concatenate([t1 * c - t2 * s, t2 * c + t1 * s], axis=-1)

    cr, cc = cos[None, :, None, :half], cos[None, :, None, half:]
    sr, sc = sin[None, :, None, :half], sin[None, :, None, half:]
    return jnp.concatenate([rot(x[..., :AXIS_ROT], cr, sr), rot(x[..., AXIS_ROT:], cc, sc)], axis=-1)


def windowed_attention(q, k, v, kc, vc, sink):
    B, L = q.shape[:2]
    LC = kc.shape[1]
    nb = L // ATTN_BLOCK
    scale = D_HEAD ** -0.5
    qb = q.reshape(B, nb, ATTN_BLOCK, ATTN_KV_HEADS, ATTN_GROUP, D_HEAD)

    def band(t):
        tp = jnp.pad(t, ((0, 0), (ATTN_BLOCK, ATTN_BLOCK), (0, 0), (0, 0)))
        parts = [tp[:, s * ATTN_BLOCK: s * ATTN_BLOCK + L].reshape(B, nb, ATTN_BLOCK, ATTN_KV_HEADS, D_HEAD)
                 for s in range(3)]
        return jnp.concatenate(parts, axis=2)

    kb, vb = band(k), band(v)
    qi = jnp.arange(ATTN_BLOCK)
    kj = jnp.arange(3 * ATTN_BLOCK) - ATTN_BLOCK
    rel = kj[None, :] - qi[:, None]
    key_pos = jnp.arange(nb)[:, None] * ATTN_BLOCK + kj[None, :]
    allowed = (jnp.abs(rel) <= WINDOW)[None] & ((key_pos >= 0) & (key_pos < L))[:, None, :]
    s_loc = jnp.einsum('bnqhgd,bnkhd->bnhgqk', qb, kb).astype(jnp.float32) * scale
    s_loc = jnp.where(allowed[None, :, None, None], s_loc, NEG_INF)
    s_ctx = jnp.einsum('bnqhgd,bchd->bnhgqc', qb, kc).astype(jnp.float32) * scale
    s_sink = jnp.broadcast_to(sink.astype(jnp.float32).reshape(ATTN_KV_HEADS, ATTN_GROUP, 1, 1),
                              s_ctx.shape[:-1] + (1,))
    p = jax.nn.softmax(jnp.concatenate([s_loc, s_ctx, s_sink], axis=-1), axis=-1).astype(q.dtype)
    nk = 3 * ATTN_BLOCK
    o = (jnp.einsum('bnhgqk,bnkhd->bnqhgd', p[..., :nk], vb)
         + jnp.einsum('bnhgqc,bchd->bnqhgd', p[..., nk:nk + LC], vc))
    return o.reshape(B, L, ATTN_Q)


def context_attention(q, k, v, sink):
    B, LC = q.shape[:2]
    qg = q.reshape(B, LC, ATTN_KV_HEADS, ATTN_GROUP, D_HEAD)
    s = jnp.einsum('bqhgd,bkhd->bhgqk', qg, k).astype(jnp.float32) * D_HEAD ** -0.5
    s_sink = jnp.broadcast_to(sink.astype(jnp.float32).reshape(1, ATTN_KV_HEADS, ATTN_GROUP, 1, 1),
                              s.shape[:-1] + (1,))
    p = jax.nn.softmax(jnp.concatenate([s, s_sink], axis=-1), axis=-1).astype(q.dtype)
    o = jnp.einsum('bhgqk,bkhd->bqhgd', p[..., :LC], v)
    return o.reshape(B, LC, ATTN_Q)


def short_conv(x, w):
    n = x.shape[1]
    pad = CONV_W // 2
    xp = jnp.pad(x, ((0, 0), (pad, pad), (0, 0)))
    y = xp[:, 0:n] * w[0]
    for j in range(1, CONV_W):
        y = y + xp[:, j:j + n] * w[j]
    return jax.nn.silu(y)


def l2norm(t):
    return t * lax.rsqrt(jnp.sum(t * t, axis=-1, keepdims=True) + EPS)


def gdn_chunk(q, k, v, g, beta, s0):
    B, L, H, DK = q.shape
    DV = v.shape[-1]
    C = GDN_CHUNK
    N = L // C

    def chunks(t):
        return jnp.moveaxis(t.reshape((B, N, C, H) + t.shape[3:]), 3, 1)

    q, k, v, g, beta = chunks(q), chunks(k), chunks(v), chunks(g), chunks(beta)
    gc = jnp.cumsum(g, axis=-1)
    idx = jnp.arange(C)
    lower = idx[:, None] >= idx[None, :]
    strict = idx[:, None] > idx[None, :]
    diff = gc[..., :, None] - gc[..., None, :]
    decay = jnp.where(lower, jnp.exp(jnp.where(lower, diff, 0.0)), 0.0)
    kb = k * beta[..., None]
    vb = v * beta[..., None]
    a = jnp.where(strict, jnp.einsum('bhnid,bhnjd->bhnij', kb, k) * decay, 0.0)
    tmat = a + jnp.eye(C, dtype=a.dtype)
    rhs = jnp.concatenate([vb, kb * jnp.exp(gc)[..., None]], axis=-1)
    sol = lax.linalg.triangular_solve(tmat, rhs, left_side=True, lower=True, unit_diagonal=True)
    u, w = sol[..., :DV], sol[..., DV:]
    qk = jnp.einsum('bhnid,bhnjd->bhnij', q, k) * decay
    q_dec = q * jnp.exp(gc)[..., None]
    k_dec = k * jnp.exp(gc[..., -1:] - gc)[..., None]
    g_last = jnp.exp(gc[..., -1])
    xs = tuple(jnp.moveaxis(t, 2, 0) for t in (q_dec, k_dec, u, w, qk, g_last))

    def step(s, inp):
        qd, kd, ui, wi, qki, gl = inp
        v_new = ui - jnp.einsum('bhcd,bhde->bhce', wi, s)
        o = jnp.einsum('bhcd,bhde->bhce', qd, s) + jnp.einsum('bhij,bhje->bhie', qki, v_new)
        s = s * gl[..., None, None] + jnp.einsum('bhcd,bhce->bhde', kd, v_new)
        return s, o

    s_fin, o = lax.scan(step, s0, xs)
    o = jnp.moveaxis(jnp.moveaxis(o, 0, 2), 1, 3).reshape(B, L, H, DV)
    return o, s_fin


def gated_deltanet(qkv_c, qkv_l, ab_c, ab_l, a_log, dt_bias):
    def prep(qkv, ab):
        B, T = qkv.shape[:2]
        qkv = qkv.astype(jnp.float32)
        ab = ab.astype(jnp.float32).reshape(B, T, 2, 2, GDN_HEADS)
        q, k, v = jnp.split(qkv, [GDN_QK, 2 * GDN_QK], axis=-1)
        q = l2norm(q.reshape(B, T, GDN_HEADS, GDN_DK)) * GDN_DK ** -0.5
        k = l2norm(k.reshape(B, T, GDN_HEADS, GDN_DK))
        v = v.reshape(B, T, GDN_HEADS, GDN_DV)
        g = -jnp.exp(a_log.astype(jnp.float32)) * jax.nn.softplus(ab[:, :, :, 0] + dt_bias.astype(jnp.float32))
        beta = jax.nn.sigmoid(ab[:, :, :, 1])
        return q, k, v, g, beta

    pc, pl = prep(qkv_c, ab_c), prep(qkv_l, ab_l)
    B = qkv_c.shape[0]
    s0 = jnp.zeros((B, GDN_HEADS, GDN_DK, GDN_DV), jnp.float32)
    outs_c, outs_l = [], []
    for d in range(2):
        def sel(p):
            q, k, v, g, beta = p
            t = (q, k, v, g[:, :, d], beta[:, :, d])
            return tuple(jnp.flip(a, axis=1) for a in t) if d == 1 else t
        oc, sc = gdn_chunk(*sel(pc), s0)
        ol, _ = gdn_chunk(*sel(pl), sc)
        if d == 1:
            oc, ol = jnp.flip(oc, axis=1), jnp.flip(ol, axis=1)
        outs_c.append(oc)
        outs_l.append(ol)
    return outs_c[0] + outs_c[1], outs_l[0] + outs_l[1]


def gated_rmsnorm(o, gate, gain):
    B, T = o.shape[:2]
    y = o * lax.rsqrt(jnp.mean(o * o, axis=-1, keepdims=True) + EPS) * gain.astype(jnp.float32)
    y = y * jax.nn.silu(gate.astype(jnp.float32)).reshape(B, T, GDN_HEADS, GDN_DV)
    return y.reshape(B, T, GDN_V).astype(gate.dtype)


def mixer(hc, hl, w_in, conv_w, a_log, dt_bias, gdn_norm, sink, w_out, cos, sin, last):
    B, LC = hc.shape[:2]
    z = jnp.concatenate([hc, hl], axis=1) @ w_in
    T = z.shape[1]
    aq, ak, av, gqkv, ggate, gab = jnp.split(z, IN_OFFSETS, axis=-1)
    aq = aq.reshape(B, T, ATTN_Q_HEADS, D_HEAD)
    ak = ak.reshape(B, T, ATTN_KV_HEADS, D_HEAD)
    av = av.reshape(B, T, ATTN_KV_HEADS, D_HEAD)
    q_l = axial_rope(aq[:, LC:], cos, sin)
    k_l = axial_rope(ak[:, LC:], cos, sin)
    o_attn_l = windowed_attention(q_l, k_l, av[:, LC:], ak[:, :LC], av[:, :LC], sink)
    qkv_c = short_conv(gqkv[:, :LC], conv_w)
    qkv_l = short_conv(gqkv[:, LC:], conv_w)
    o_gdn_c, o_gdn_l = gated_deltanet(qkv_c, qkv_l, gab[:, :LC], gab[:, LC:], a_log, dt_bias)
    o_gdn_l = gated_rmsnorm(o_gdn_l, ggate[:, LC:], gdn_norm)
    out_l = jnp.concatenate([o_attn_l, o_gdn_l], axis=-1) @ w_out
    if last:
        return None, out_l
    o_attn_c = context_attention(aq[:, :LC], ak[:, :LC], av[:, :LC], sink)
    o_gdn_c = gated_rmsnorm(o_gdn_c, ggate[:, :LC], gdn_norm)
    out_c = jnp.concatenate([o_attn_c, o_gdn_c], axis=-1) @ w_out
    return out_c, out_l


def squared_relu_mlp(h, w1, w2):
    return jnp.square(jax.nn.relu(h @ w1)) @ w2


def setup_inputs(seed: int = 0) -> dict:
    key = jax.random.key(seed)
    ks = jax.random.split(key, 18)
    f32 = jnp.float32

    def nrm(k, shape, scale):
        return jax.random.normal(k, shape, f32) * scale

    dt = jnp.exp(jax.random.uniform(ks[9], (DEPTH, 2, GDN_HEADS), f32, math.log(1e-3), math.log(1e-1)))
    return {
        "x": nrm(ks[0], (BATCH, SEQ, D_MODEL), 1.0),
        "c": nrm(ks[1], (BATCH, D_MODEL), 1.0),
        "ctx": nrm(ks[2], (BATCH, CTX_LEN, D_MODEL), 1.0),
        "c_ctx": nrm(ks[3], (D_MODEL,), 1.0),
        "w_ada": nrm(ks[4], (DEPTH, D_MODEL, 6 * D_MODEL), D_MODEL ** -0.5),
        "b_ada": nrm(ks[5], (DEPTH, 6 * D_MODEL), 0.02),
        "norm_mix": 1.0 + nrm(ks[6], (DEPTH, D_MODEL), 0.02),
        "w_in": nrm(ks[7], (DEPTH, D_MODEL, D_IN), D_MODEL ** -0.5),
        "conv_w": nrm(ks[8], (DEPTH, CONV_W, GDN_QKV), CONV_W ** -0.5),
        "a_log": jnp.log(jax.random.uniform(ks[10], (DEPTH, 2, GDN_HEADS), f32, 1.0, 16.0)),
        "dt_bias": dt + jnp.log(-jnp.expm1(-dt)),
        "gdn_norm": 1.0 + nrm(ks[11], (DEPTH, GDN_DV), 0.02),
        "attn_sink": nrm(ks[12], (DEPTH, ATTN_Q_HEADS), 0.5),
        "w_out": nrm(ks[13], (DEPTH, MIX_OUT, D_MODEL), MIX_OUT ** -0.5),
        "norm_ffn": 1.0 + nrm(ks[14], (DEPTH, D_MODEL), 0.02),
        "w_ff1": nrm(ks[15], (DEPTH, D_MODEL, D_FF), D_MODEL ** -0.5),
        "w_ff2": nrm(ks[16], (DEPTH, D_FF, D_MODEL), D_FF ** -0.5),
        "norm_final": 1.0 + nrm(ks[17], (D_MODEL,), 0.02),
    }


def reference(x, c, ctx, c_ctx, w_ada, b_ada, norm_mix, w_in, conv_w, a_log, dt_bias, gdn_norm,
              attn_sink, w_out, norm_ffn, w_ff1, w_ff2, norm_final):
    L = x.shape[1]
    cos, sin = axial_rope_tables(L, x.dtype)
    silu_c = jax.nn.silu(c)
    silu_cc = jax.nn.silu(c_ctx)
    xl, xc = x, ctx
    for i in range(DEPTH):
        last = i == DEPTH - 1
        mod_l = (silu_c @ w_ada[i] + b_ada[i])[:, None, :]
        mod_c = silu_cc @ w_ada[i] + b_ada[i]
        shm_l, scm_l, gm_l, shf_l, scf_l, gf_l = jnp.split(mod_l, 6, axis=-1)
        shm_c, scm_c, gm_c, shf_c, scf_c, gf_c = jnp.split(mod_c, 6, axis=-1)
        hl = modulate(rmsnorm(xl, norm_mix[i]), shm_l, scm_l)
        hc = modulate(rmsnorm(xc, norm_mix[i]), shm_c, scm_c)
        out_c, out_l = mixer(hc, hl, w_in[i], conv_w[i], a_log[i], dt_bias[i], gdn_norm[i],
                             attn_sink[i], w_out[i], cos, sin, last)
        xl = xl + gm_l * out_l
        xl = xl + gf_l * squared_relu_mlp(modulate(rmsnorm(xl, norm_ffn[i]), shf_l, scf_l), w_ff1[i], w_ff2[i])
        if not last:
            xc = xc + gm_c * out_c
            xc = xc + gf_c * squared_relu_mlp(modulate(rmsnorm(xc, norm_ffn[i]), shf_c, scf_c),
                                              w_ff1[i], w_ff2[i])
    return rmsnorm(xl, norm_final)
```

```python
import functools
import math

import jax
import jax.numpy as jnp
from jax import lax
from jax.experimental import pallas as pl
from jax.experimental.pallas import tpu as pltpu

F32 = jnp.float32
BF16 = jnp.bfloat16

D_MODEL = 2048
D_HEAD = 128
ATTN_Q_HEADS = 8
ATTN_KV_HEADS = 2
ATTN_GROUP = ATTN_Q_HEADS // ATTN_KV_HEADS
WINDOW = 128
ATTN_BLOCK = 128
GDN_HEADS = 8
GDN_DK = 128
GDN_DV = 128
CONV_W = 5
D_FF = 4 * D_MODEL
GRID_W = 64
ROPE_BASE = 10000.0
AXIS_ROT = D_HEAD // 2
EPS = 1e-6
NEG_INF = -1e30

ATTN_Q = ATTN_Q_HEADS * D_HEAD
ATTN_KV = ATTN_KV_HEADS * D_HEAD
GDN_QK = GDN_HEADS * GDN_DK
GDN_V = GDN_HEADS * GDN_DV
GDN_AB = 4 * GDN_HEADS
COL_AQ = 0
COL_AK = ATTN_Q
COL_AV = COL_AK + ATTN_KV
COL_GQ = COL_AV + ATTN_KV
COL_GK = COL_GQ + GDN_QK
COL_GV = COL_GK + GDN_QK
COL_GATE = COL_GV + GDN_V
COL_AB = COL_GATE + GDN_V
D_MAIN = COL_AB
MIX_HALF = ATTN_Q

LANES = 128
SUBLANES = 8
VMEM_LIMIT_BYTES = 56 * 1024 * 1024

GDN_CHUNK = 128
GDN_INV_BASE = 16
GDN_HEADS_PER_STEP = 2
CONV_ROWS = 256
GATE_ROWS = 256
TM = 768
TN_IN = 512
TF = 512
TN_ADA = 1024
OUT_COLS = 512


def _cparams(sem):
    return pltpu.CompilerParams(dimension_semantics=sem, vmem_limit_bytes=VMEM_LIMIT_BYTES)


def _split3(x):
    hi = x.astype(BF16)
    r1 = x - hi.astype(F32)
    mid = r1.astype(BF16)
    lo = (r1 - mid.astype(F32)).astype(BF16)
    return hi, mid, lo


def _dot(a, b):
    return jnp.dot(a, b, preferred_element_type=F32)


def _dot_nt(a, b):
    return lax.dot_general(a, b, (((1,), (1,)), ((), ())), preferred_element_type=F32)


def _dot_tn(a, b):
    return lax.dot_general(a, b, (((0,), (0,)), ((), ())), preferred_element_type=F32)


def _ada_kernel(s_ref, w_ref, b_ref, o_ref):
    s = s_ref[...]
    s = s * jax.nn.sigmoid(s)
    o_ref[...] = _dot(s.astype(BF16), w_ref[...].astype(BF16)) + b_ref[...]


def _ada_modulation(cond, w_ada, b_ada):
    depth = w_ada.shape[0]
    n = w_ada.shape[2]
    return pl.pallas_call(
        _ada_kernel,
        grid=(depth, n // TN_ADA),
        in_specs=[
            pl.BlockSpec((SUBLANES, D_MODEL), lambda l, j: (0, 0)),
            pl.BlockSpec((None, D_MODEL, TN_ADA), lambda l, j: (l, 0, j)),
            pl.BlockSpec((None, 1, TN_ADA), lambda l, j: (l, 0, j)),
        ],
        out_specs=pl.BlockSpec((None, SUBLANES, TN_ADA), lambda l, j: (l, 0, j)),
        out_shape=jax.ShapeDtypeStruct((depth, SUBLANES, n), F32),
        compiler_params=_cparams(("arbitrary", "arbitrary")),
        name="ada_mod",
    )(cond, w_ada, b_ada.reshape(depth, 1, n))


def _norm_modulate(x, gain, modc, modl, shift_row, is_ctx):
    y = x * lax.rsqrt(jnp.mean(x * x, axis=-1, keepdims=True) + EPS)
    y = y * gain
    shift = jnp.where(is_ctx, modc[shift_row:shift_row + 1, :], modl[shift_row:shift_row + 1, :])
    scale = jnp.where(is_ctx, modc[shift_row + 1:shift_row + 2, :], modl[shift_row + 1:shift_row + 2, :])
    return y * (1.0 + scale) + shift


def _is_ctx_rows(tile_idx, tiles_per_batch, tm, lc):
    row = (tile_idx % tiles_per_batch) * tm + lax.broadcasted_iota(jnp.int32, (tm, 1), 0)
    return row < lc


def _inproj_kernel(x_ref, g_ref, modc_ref, modl_ref, w_ref, wab_ref, z_ref, zab_ref, xn_ref, *,
                   tm, lc, tiles_per_batch):
    i = pl.program_id(0)
    j = pl.program_id(1)

    @pl.when(j == 0)
    def _():
        is_ctx = _is_ctx_rows(i, tiles_per_batch, tm, lc)
        h = _norm_modulate(x_ref[...], g_ref[...], modc_ref[...], modl_ref[...], 0, is_ctx)
        hb = h.astype(BF16)
        xn_ref[...] = hb
        zab_ref[...] = _dot(hb, wab_ref[...])

    z_ref[...] = _dot(xn_ref[...], w_ref[...])


def _in_projection(x, gain, mod, w_main, w_ab, *, batch, lc):
    rows = x.shape[0]
    t = rows // batch
    tm = TM
    tpb = t // tm
    kern = functools.partial(_inproj_kernel, tm=tm, lc=lc, tiles_per_batch=tpb)
    return pl.pallas_call(
        kern,
        grid=(rows // tm, D_MAIN // TN_IN),
        in_specs=[
            pl.BlockSpec((tm, D_MODEL), lambda i, j: (i, 0)),
            pl.BlockSpec((1, D_MODEL), lambda i, j: (0, 0)),
            pl.BlockSpec((None, 6, D_MODEL), lambda i, j: (batch, 0, 0)),
            pl.BlockSpec((None, 6, D_MODEL), lambda i, j: (i // tpb, 0, 0)),
            pl.BlockSpec((D_MODEL, TN_IN), lambda i, j: (0, j)),
            pl.BlockSpec((D_MODEL, LANES), lambda i, j: (0, 0)),
        ],
        out_specs=[
            pl.BlockSpec((tm, TN_IN), lambda i, j: (i, j)),
            pl.BlockSpec((tm, LANES), lambda i, j: (i, 0)),
        ],
        out_shape=[
            jax.ShapeDtypeStruct((rows, D_MAIN), F32),
            jax.ShapeDtypeStruct((rows, LANES), F32),
        ],
        scratch_shapes=[pltpu.VMEM((tm, D_MODEL), BF16)],
        compiler_params=_cparams(("arbitrary", "arbitrary")),
        name="in_proj",
    )(x, gain, mod, mod, w_main, w_ab)


def _gates_kernel(ab_ref, alog_ref, dt_ref, o_ref, *, chunk):
    n = ab_ref.shape[0]
    ab = ab_ref[...]
    lane = lax.broadcasted_iota(jnp.int32, (1, LANES), 1)
    is_decay = ((lane & 15) < 8) & (lane < GDN_AB)
    g = -jnp.exp(alog_ref[...]) * jax.nn.softplus(ab + dt_ref[...])
    g = jnp.where(is_decay, g, 0.0)
    beta = jax.nn.sigmoid(ab)
    r = lax.broadcasted_iota(jnp.int32, (n, n), 0)
    c = lax.broadcasted_iota(jnp.int32, (n, n), 1)
    shift = int(math.log2(chunk))
    same = lax.shift_right_logical(r, shift) == lax.shift_right_logical(c, shift)
    lower = jnp.where(same & (c <= r), 1.0, 0.0).astype(BF16)
    upper = jnp.where(same & (c >= r), 1.0, 0.0).astype(BF16)
    hi, mid, lo = _split3(g)
    cum_f = _dot(lower, hi) + _dot(lower, mid) + _dot(lower, lo)
    cum_r = _dot(upper, hi) + _dot(upper, mid) + _dot(upper, lo)
    o_ref[...] = jnp.where(is_decay, jnp.where(lane < 16, cum_f, cum_r), beta)


def _gdn_gates(zab, alog_vec, dt_vec):
    rows = zab.shape[0]
    kern = functools.partial(_gates_kernel, chunk=GDN_CHUNK)
    return pl.pallas_call(
        kern,
        grid=(rows // GATE_ROWS,),
        in_specs=[
            pl.BlockSpec((GATE_ROWS, LANES), lambda i: (i, 0)),
            pl.BlockSpec((1, LANES), lambda i: (0, 0)),
            pl.BlockSpec((1, LANES), lambda i: (0, 0)),
        ],
        out_specs=pl.BlockSpec((GATE_ROWS, LANES), lambda i: (i, 0)),
        out_shape=jax.ShapeDtypeStruct((rows, LANES), F32),
        compiler_params=_cparams(("arbitrary",)),
        name="gdn_gates",
    )(zab, alog_vec, dt_vec)


def _unit_tri_inverse(a, i_idx, j_idx):
    c = a.shape[0]
    base_shift = int(math.log2(GDN_INV_BASE))
    eye = jnp.where(i_idx == j_idx, 1.0, 0.0)
    in_base = lax.shift_right_logical(i_idx, base_shift) == lax.shift_right_logical(j_idx, base_shift)
    p = jnp.where(in_base, -a, 0.0)
    x = eye + p
    for _ in range(base_shift - 1):
        pb = p.astype(BF16)
        p = _dot(pb, pb)
        x = x + _dot(x.astype(BF16), p.astype(BF16))
    size = GDN_INV_BASE
    while size < c:
        s = int(math.log2(size))
        same_pair = lax.shift_right_logical(i_idx, s + 1) == lax.shift_right_logical(j_idx, s + 1)
        same_blk = lax.shift_right_logical(i_idx, s) == lax.shift_right_logical(j_idx, s)
        off = jnp.where(same_pair & jnp.logical_not(same_blk), a, 0.0)
        xb = x.astype(BF16)
        x = x - _dot(_dot(xb, off.astype(BF16)).astype(BF16), xb)
        size *= 2
    return x


def _gdn_kernel(zq_ref, zk_ref, zv_ref, zg_ref, cwq_ref, cwk_ref, cwv_ref, gc_ref, gnorm_ref, o_ref,
                xpad_ref, q_s, k_s, v_s, of_s, or_s, gsel_s, gselt_s, state_s, *, lc, hpg):
    t = zq_ref.shape[0]
    lat = t - lc
    c = GDN_CHUNK
    n_chunks = t // c
    n_ctx_chunks = lc // c
    hg = pl.program_id(1)

    def selector(src_axis):
        src_lane = lax.broadcasted_iota(jnp.int32, (LANES, LANES), src_axis)
        dst_lane = lax.broadcasted_iota(jnp.int32, (LANES, LANES), 1 - src_axis)
        kind = dst_lane & 3
        want = lax.shift_right_logical(kind, 1) * 16 + (kind & 1) * 8 + hg * hpg + lax.shift_right_logical(dst_lane, 2)
        return jnp.where((src_lane == want) & (dst_lane < 4 * hpg), 1.0, 0.0).astype(BF16)

    sel = selector(0)
    sel_t = selector(1)

    def gather_rows(r, _):
        rows = pl.ds(pl.multiple_of(r * c, c), c)
        hi, mid, lo = _split3(gc_ref[rows, :])
        gsel_s[rows, :] = _dot(hi, sel) + _dot(mid, sel) + _dot(lo, sel)
        gselt_s[r] = _dot_nt(sel_t, hi) + _dot_nt(sel_t, mid) + _dot_nt(sel_t, lo)
        return 0

    lax.fori_loop(0, n_chunks, gather_rows, 0)

    zeros8 = jnp.zeros((SUBLANES, LANES), F32)
    n_conv_blocks = t // CONV_ROWS
    win = CONV_ROWS + 2 * SUBLANES

    def conv_stream(src_ref, w_ref, dst_ref, hh, kind):
        cols = slice(hh * LANES, (hh + 1) * LANES)
        xpad_ref[0:SUBLANES, :] = zeros8
        xpad_ref[SUBLANES:SUBLANES + lc, :] = src_ref[0:lc, cols]
        xpad_ref[SUBLANES + lc:2 * SUBLANES + lc, :] = zeros8
        xpad_ref[2 * SUBLANES + lc:2 * SUBLANES + t, :] = src_ref[lc:t, cols]
        xpad_ref[2 * SUBLANES + t:3 * SUBLANES + t, :] = zeros8
        w = w_ref[:, cols]

        def body(d, _):
            ws = pl.multiple_of(d * CONV_ROWS + SUBLANES * jnp.minimum(d, 1), SUBLANES)
            xw = xpad_ref[pl.ds(ws, win), :]
            acc = None
            for j in range(CONV_W):
                shift = (CONV_W // 2 - j) % win
                xs = xw if shift == 0 else pltpu.roll(xw, shift, 0)
                term = xs[SUBLANES:SUBLANES + CONV_ROWS, :] * w[j:j + 1, :]
                acc = term if acc is None else acc + term
            y = acc * jax.nn.sigmoid(acc)
            if kind != "v":
                y = y * lax.rsqrt(jnp.sum(y * y, axis=-1, keepdims=True) + EPS)
            if kind == "q":
                y = y * (GDN_DK ** -0.5)
            dst_ref[hh, pl.ds(pl.multiple_of(d * CONV_ROWS, CONV_ROWS), CONV_ROWS), :] = y
            return 0

        lax.fori_loop(0, n_conv_blocks, body, 0)

    for hh in range(hpg):
        conv_stream(zq_ref, cwq_ref, q_s, hh, "q")
        conv_stream(zk_ref, cwk_ref, k_s, hh, "k")
        conv_stream(zv_ref, cwv_ref, v_s, hh, "v")

    i_idx = lax.broadcasted_iota(jnp.int32, (c, c), 0)
    j_idx = lax.broadcasted_iota(jnp.int32, (c, c), 1)

    def chunk_step(hh, d, ci):
        rows = pl.ds(pl.multiple_of(ci * c, c), c)
        q = q_s[hh, rows, :]
        k = k_s[hh, rows, :]
        v = v_s[hh, rows, :]
        l0 = 4 * hh + 2 * d
        gcol = gsel_s[rows, :]
        gc_i = jnp.broadcast_to(gcol[:, l0:l0 + 1], (c, c))
        beta_i = jnp.broadcast_to(gcol[:, l0 + 1:l0 + 2], (c, c))
        gc_j = jnp.broadcast_to(gselt_s[ci, l0:l0 + 1, :], (c, c))
        incl = (i_idx >= j_idx) if d == 0 else (i_idx <= j_idx)
        strict = (i_idx > j_idx) if d == 0 else (i_idx < j_idx)
        dec = jnp.where(incl, jnp.exp(jnp.where(incl, gc_i - gc_j, 0.0)), 0.0)
        kb16 = k.astype(BF16)
        qk = _dot_nt(q.astype(BF16), kb16)
        kbeta = k * beta_i
        a = jnp.where(strict, _dot_nt(kbeta.astype(BF16), kb16) * dec, 0.0)
        tinv = _unit_tri_inverse(a, i_idx, j_idx)
        eg = jnp.exp(gc_i)
        last = c - 1 if d == 0 else 0
        tot = jnp.broadcast_to(gc_i[last:last + 1, :], (c, c))
        rhs = jnp.concatenate([v * beta_i, kbeta * eg], axis=1).astype(BF16)
        sol = _dot(tinv.astype(BF16), rhs)
        u = sol[:, :GDN_DV]
        w = sol[:, GDN_DV:]
        qd = q * eg
        kd = k * jnp.exp(tot - gc_i)
        qkd = qk * dec
        sidx = 2 * hh + d
        s = state_s[sidx]
        s16 = s.astype(BF16)
        ws = _dot(jnp.concatenate([w, qd], axis=0).astype(BF16), s16)
        v_new = u - ws[:c]
        vn16 = v_new.astype(BF16)
        o = ws[c:] + _dot(qkd.astype(BF16), vn16)
        state_s[sidx] = s * jnp.exp(tot) + _dot_tn(kd.astype(BF16), vn16)
        if d == 0:
            of_s[hh, rows, :] = o
        else:
            or_s[hh, rows, :] = o

    state_s[...] = jnp.zeros(state_s.shape, F32)

    def ctx_body(s, _):
        for hh in range(hpg):
            chunk_step(hh, 0, s)
            chunk_step(hh, 1, n_ctx_chunks - 1 - s)
        return 0

    def lat_body(s, _):
        for hh in range(hpg):
            chunk_step(hh, 0, n_ctx_chunks + s)
            chunk_step(hh, 1, n_chunks - 1 - s)
        return 0

    lax.fori_loop(0, n_ctx_chunks, ctx_body, 0)
    lax.fori_loop(0, n_chunks - n_ctx_chunks, lat_body, 0)

    gain = gnorm_ref[...]

    def out_body(r, _):
        rows = pl.ds(pl.multiple_of(r * CONV_ROWS, CONV_ROWS), CONV_ROWS)
        for hh in range(hpg):
            cols = slice(hh * LANES, (hh + 1) * LANES)
            o = of_s[hh, rows, :] + or_s[hh, rows, :]
            y = o * lax.rsqrt(jnp.mean(o * o, axis=-1, keepdims=True) + EPS) * gain
            gate = zg_ref[rows, cols]
            y = y * (gate * jax.nn.sigmoid(gate))
            o_ref[rows, cols] = y.astype(o_ref.dtype)
        return 0

    lax.fori_loop(0, t // CONV_ROWS, out_body, 0)


def _gated_deltanet(z, conv_w, gcum, gnorm, *, batch, lc):
    rows = z.shape[0]
    t = rows // batch
    hpg = GDN_HEADS_PER_STEP
    wblk = hpg * LANES
    groups = GDN_HEADS // hpg
    kern = functools.partial(_gdn_kernel, lc=lc, hpg=hpg)

    def zspec(col0):
        return pl.BlockSpec((t, wblk), lambda b, g: (b, col0 // wblk + g))

    def wspec(col0):
        return pl.BlockSpec((CONV_W, wblk), lambda b, g: (0, col0 // wblk + g))

    n_chunks = t // GDN_CHUNK
    return pl.pallas_call(
        kern,
        grid=(batch, groups),
        in_specs=[
            zspec(COL_GQ), zspec(COL_GK), zspec(COL_GV), zspec(COL_GATE),
            wspec(0), wspec(GDN_QK), wspec(2 * GDN_QK),
            pl.BlockSpec((t, LANES), lambda b, g: (b, 0)),
            pl.BlockSpec((1, LANES), lambda b, g: (0, 0)),
        ],
        out_specs=pl.BlockSpec((t, wblk), lambda b, g: (b, g)),
        out_shape=jax.ShapeDtypeStruct((rows, GDN_V), BF16),
        scratch_shapes=[
            pltpu.VMEM((t + 3 * SUBLANES, LANES), F32),
            pltpu.VMEM((hpg, t, LANES), F32),
            pltpu.VMEM((hpg, t, LANES), F32),
            pltpu.VMEM((hpg, t, LANES), F32),
            pltpu.VMEM((hpg, t, LANES), F32),
            pltpu.VMEM((hpg, t, LANES), F32),
            pltpu.VMEM((t, LANES), F32),
            pltpu.VMEM((n_chunks, LANES, GDN_CHUNK), F32),
            pltpu.VMEM((2 * hpg, GDN_DK, GDN_DV), F32),
        ],
        compiler_params=_cparams(("arbitrary", "arbitrary")),
        name="gdn",
    )(z, z, z, z, conv_w, conv_w, conv_w, gcum, gnorm)


def _attn_kernel(sink_ref, q_ref, k_ref, v_ref, cos_ref, sin_ref, o_ref, kp_ref, vp_ref, *, lc):
    t = q_ref.shape[0]
    lat = t - lc
    blk = ATTN_BLOCK
    nb = lat // blk
    hk = pl.program_id(1)
    scale = D_HEAD ** -0.5
    grp = ATTN_GROUP

    lane = lax.broadcasted_iota(jnp.int32, (1, D_HEAD), 1)
    first_half = (lane & (AXIS_ROT - 1)) < (AXIS_ROT // 2)

    def rope(x, cos, sin):
        swapped = jnp.where(first_half, pltpu.roll(x, D_HEAD - AXIS_ROT // 2, 1),
                            pltpu.roll(x, AXIS_ROT // 2, 1))
        return x * cos + swapped * sin

    def sink_col(rows_per_head):
        return jnp.concatenate(
            [jnp.full((rows_per_head, 1), sink_ref[hk * grp + g], F32) for g in range(grp)], axis=0)

    kc = k_ref[0:lc, :].astype(BF16)
    vc = v_ref[0:lc, :].astype(BF16)

    qc = jnp.concatenate([q_ref[0:lc, g * D_HEAD:(g + 1) * D_HEAD].astype(BF16) for g in range(grp)],
                         axis=0)
    s = _dot_nt(qc, kc) * scale
    sk = sink_col(lc)
    m = jnp.maximum(jnp.max(s, axis=-1, keepdims=True), sk)
    p = jnp.exp(s - m)
    den = jnp.sum(p, axis=-1, keepdims=True) + jnp.exp(sk - m)
    oc = _dot(p.astype(BF16), vc) / den
    for g in range(grp):
        o_ref[0:lc, g * D_HEAD:(g + 1) * D_HEAD] = oc[g * lc:(g + 1) * lc].astype(o_ref.dtype)

    zpad = jnp.zeros((blk, D_HEAD), BF16)
    kp_ref[0:blk, :] = zpad
    vp_ref[0:blk, :] = zpad
    kp_ref[blk + lat:2 * blk + lat, :] = zpad
    vp_ref[blk + lat:2 * blk + lat, :] = zpad

    def fill(n, _):
        src = pl.ds(pl.multiple_of(lc + n * blk, blk), blk)
        pos = pl.ds(pl.multiple_of(n * blk, blk), blk)
        dst = pl.ds(pl.multiple_of(blk + n * blk, blk), blk)
        kp_ref[dst, :] = rope(k_ref[src, :], cos_ref[pos, :], sin_ref[pos, :]).astype(BF16)
        vp_ref[dst, :] = v_ref[src, :].astype(BF16)
        return 0

    lax.fori_loop(0, nb, fill, 0)

    qi = lax.broadcasted_iota(jnp.int32, (grp * blk, 3 * blk), 0) & (blk - 1)
    col = lax.broadcasted_iota(jnp.int32, (grp * blk, 3 * blk), 1)
    rel = col - blk - qi
    in_window = jnp.abs(rel) <= WINDOW
    sk_l = sink_col(blk)

    def block(n, _):
        src = pl.ds(pl.multiple_of(lc + n * blk, blk), blk)
        pos = pl.ds(pl.multiple_of(n * blk, blk), blk)
        cos = cos_ref[pos, :]
        sin = sin_ref[pos, :]
        qs = [rope(q_ref[src, g * D_HEAD:(g + 1) * D_HEAD], cos, sin).astype(BF16) for g in range(grp)]
        qb = jnp.concatenate(qs, axis=0)
        band = pl.ds(pl.multiple_of(n * blk, blk), 3 * blk)
        kb = kp_ref[band, :]
        vb = vp_ref[band, :]
        s_loc = _dot_nt(qb, kb) * scale
        s_ctx = _dot_nt(qb, kc) * scale
        key_pos = n * blk + col - blk
        allowed = in_window & (key_pos >= 0) & (key_pos < lat)
        s_loc = jnp.where(allowed, s_loc, NEG_INF)
        m = jnp.maximum(jnp.maximum(jnp.max(s_loc, axis=-1, keepdims=True),
                                    jnp.max(s_ctx, axis=-1, keepdims=True)), sk_l)
        p_loc = jnp.exp(s_loc - m)
        p_ctx = jnp.exp(s_ctx - m)
        den = (jnp.sum(p_loc, axis=-1, keepdims=True) + jnp.sum(p_ctx, axis=-1, keepdims=True)
               + jnp.exp(sk_l - m))
        o = (_dot(p_loc.astype(BF16), vb) + _dot(p_ctx.astype(BF16), vc)) / den
        for g in range(grp):
            o_ref[src, g * D_HEAD:(g + 1) * D_HEAD] = o[g * blk:(g + 1) * blk].astype(o_ref.dtype)
        return 0

    lax.fori_loop(0, nb, block, 0)


def _windowed_attention(z, sink, cos, sin, *, batch, lc):
    rows = z.shape[0]
    t = rows // batch
    lat = t - lc
    qw = ATTN_GROUP * D_HEAD
    kern = functools.partial(_attn_kernel, lc=lc)
    return pl.pallas_call(
        kern,
        grid=(batch, ATTN_KV_HEADS),
        in_specs=[
            pl.BlockSpec(memory_space=pltpu.SMEM),
            pl.BlockSpec((t, qw), lambda b, h: (b, h)),
            pl.BlockSpec((t, D_HEAD), lambda b, h: (b, COL_AK // D_HEAD + h)),
            pl.BlockSpec((t, D_HEAD), lambda b, h: (b, COL_AV // D_HEAD + h)),
            pl.BlockSpec((lat, D_HEAD), lambda b, h: (0, 0)),
            pl.BlockSpec((lat, D_HEAD), lambda b, h: (0, 0)),
        ],
        out_specs=pl.BlockSpec((t, qw), lambda b, h: (b, h)),
        out_shape=jax.ShapeDtypeStruct((rows, ATTN_Q), BF16),
        scratch_shapes=[
            pltpu.VMEM((lat + 2 * ATTN_BLOCK, D_HEAD), BF16),
            pltpu.VMEM((lat + 2 * ATTN_BLOCK, D_HEAD), BF16),
        ],
        compiler_params=_cparams(("arbitrary", "arbitrary")),
        name="win_attn",
    )(sink, z, z, z, cos, sin)


def _outproj_kernel(x_ref, a_ref, g_ref, wa_ref, wg_ref, modc_ref, modl_ref, o_ref, *, tm, lc,
                    tiles_per_batch):
    i = pl.program_id(0)
    is_ctx = _is_ctx_rows(i, tiles_per_batch, tm, lc)
    a = a_ref[...]
    g = g_ref[...]
    for n0 in range(0, D_MODEL, OUT_COLS):
        cols = slice(n0, n0 + OUT_COLS)
        acc = _dot(a, wa_ref[:, cols]) + _dot(g, wg_ref[:, cols])
        gate = jnp.where(is_ctx, modc_ref[2:3, cols], modl_ref[2:3, cols])
        o_ref[:, cols] = x_ref[:, cols] + gate * acc


def _out_projection(x, attn, gdn, w_out, mod, *, batch, lc):
    rows = x.shape[0]
    t = rows // batch
    tm = TM
    tpb = t // tm
    kern = functools.partial(_outproj_kernel, tm=tm, lc=lc, tiles_per_batch=tpb)
    return pl.pallas_call(
        kern,
        grid=(rows // tm,),
        in_specs=[
            pl.BlockSpec((tm, D_MODEL), lambda i: (i, 0)),
            pl.BlockSpec((tm, MIX_HALF), lambda i: (i, 0)),
            pl.BlockSpec((tm, MIX_HALF), lambda i: (i, 0)),
            pl.BlockSpec((MIX_HALF, D_MODEL), lambda i: (0, 0), pipeline_mode=pl.Buffered(1)),
            pl.BlockSpec((MIX_HALF, D_MODEL), lambda i: (1, 0), pipeline_mode=pl.Buffered(1)),
            pl.BlockSpec((None, 6, D_MODEL), lambda i: (batch, 0, 0)),
            pl.BlockSpec((None, 6, D_MODEL), lambda i: (i // tpb, 0, 0)),
        ],
        out_specs=pl.BlockSpec((tm, D_MODEL), lambda i: (i, 0)),
        out_shape=jax.ShapeDtypeStruct((rows, D_MODEL), F32),
        compiler_params=_cparams(("arbitrary",)),
        name="out_proj",
    )(x, attn, gdn, w_out, w_out, mod, mod)


def _ffn_kernel(x_ref, g_ref, modc_ref, modl_ref, w1_ref, w2_ref, gfin_ref, o_ref, xn_ref, *, tm, lc,
                tiles_per_batch, n_f, final_norm):
    i = pl.program_id(0)
    j = pl.program_id(1)
    is_ctx = _is_ctx_rows(i, tiles_per_batch, tm, lc)

    @pl.when(j == 0)
    def _():
        h = _norm_modulate(x_ref[...], g_ref[...], modc_ref[...], modl_ref[...], 3, is_ctx)
        xn_ref[...] = h.astype(BF16)

        o_ref[...] = jnp.zeros(o_ref.shape, F32)

    h = jnp.maximum(_dot(xn_ref[...], w1_ref[...]), 0.0)
    h16 = (h * h).astype(BF16)
    for n0 in range(0, D_MODEL, OUT_COLS):
        cols = slice(n0, n0 + OUT_COLS)
        o_ref[:, cols] += _dot(h16, w2_ref[:, cols])

    @pl.when(j == n_f - 1)
    def _():
        gate = jnp.where(is_ctx, modc_ref[5:6, :], modl_ref[5:6, :])
        y = x_ref[...] + gate * o_ref[...]
        if final_norm:
            y = y * lax.rsqrt(jnp.mean(y * y, axis=-1, keepdims=True) + EPS) * gfin_ref[...]
        o_ref[...] = y


def _ffn(x, gain, mod, w1, w2, gfin, *, batch, lc, final_norm):
    rows = x.shape[0]
    t = rows // batch
    tm = TM
    tpb = t // tm
    n_f = D_FF // TF
    kern = functools.partial(_ffn_kernel, tm=tm, lc=lc, tiles_per_batch=tpb, n_f=n_f,
                             final_norm=final_norm)
    return pl.pallas_call(
        kern,
        grid=(rows // tm, n_f),
        in_specs=[
            pl.BlockSpec((tm, D_MODEL), lambda i, j: (i, 0), pipeline_mode=pl.Buffered(1)),
            pl.BlockSpec((1, D_MODEL), lambda i, j: (0, 0)),
            pl.BlockSpec((None, 6, D_MODEL), lambda i, j: (batch, 0, 0)),
            pl.BlockSpec((None, 6, D_MODEL), lambda i, j: (i // tpb, 0, 0)),
            pl.BlockSpec((D_MODEL, TF), lambda i, j: (0, j)),
            pl.BlockSpec((TF, D_MODEL), lambda i, j: (j, 0)),
            pl.BlockSpec((1, D_MODEL), lambda i, j: (0, 0)),
        ],
        out_specs=pl.BlockSpec((tm, D_MODEL), lambda i, j: (i, 0)),
        out_shape=jax.ShapeDtypeStruct((rows, D_MODEL), F32),
        scratch_shapes=[pltpu.VMEM((tm, D_MODEL), BF16)],
        compiler_params=_cparams(("arbitrary", "arbitrary")),
        name="ffn",
    )(x, gain, mod, mod, w1, w2, gfin)


def _rope_tables(n_tokens):
    rows = n_tokens // GRID_W
    row = jnp.repeat(jnp.arange(rows, dtype=F32), GRID_W)
    col = jnp.tile(jnp.arange(GRID_W, dtype=F32), rows)
    inv_freq = ROPE_BASE ** (-jnp.arange(0, AXIS_ROT, 2, dtype=F32) / AXIS_ROT)
    ang_r = row[:, None] * inv_freq
    ang_c = col[:, None] * inv_freq
    cos = jnp.concatenate([jnp.cos(ang_r)] * 2 + [jnp.cos(ang_c)] * 2, axis=-1)
    sin = jnp.concatenate([-jnp.sin(ang_r), jnp.sin(ang_r), -jnp.sin(ang_c), jnp.sin(ang_c)], axis=-1)
    return cos, sin


def _gate_lane_vector(p):
    v = jnp.zeros((2, 2, GDN_HEADS), F32).at[:, 0, :].set(p.astype(F32)).reshape(1, GDN_AB)
    return jnp.pad(v, ((0, 0), (0, LANES - GDN_AB)))


def kernel(x, c, ctx, c_ctx, w_ada, b_ada, norm_mix, w_in, conv_w, a_log, dt_bias, gdn_norm, attn_sink,
           w_out, norm_ffn, w_ff1, w_ff2, norm_final):
    batch, lat, d = x.shape
    lc = ctx.shape[1]
    depth = w_in.shape[0]
    t = lc + lat
    assert d == D_MODEL and batch < SUBLANES
    assert t % TM == 0 and lc % GDN_CHUNK == 0 and lat % CONV_ROWS == 0 and lc % CONV_ROWS == 0

    cond = jnp.zeros((SUBLANES, D_MODEL), F32).at[:batch].set(c).at[batch].set(c_ctx)
    mod = _ada_modulation(cond, w_ada, b_ada).reshape(depth, SUBLANES, 6, D_MODEL)

    cos, sin = _rope_tables(lat)
    xs = jnp.concatenate([ctx, x], axis=1).reshape(batch * t, D_MODEL)
    gfin = norm_final.reshape(1, D_MODEL)

    for i in range(depth):
        w_main = w_in[i, :, :D_MAIN].astype(BF16)
        w_ab = jnp.pad(w_in[i, :, D_MAIN:], ((0, 0), (0, LANES - GDN_AB))).astype(BF16)
        z, zab = _in_projection(xs, norm_mix[i].reshape(1, D_MODEL), mod[i], w_main, w_ab,
                                batch=batch, lc=lc)
        gcum = _gdn_gates(zab, _gate_lane_vector(a_log[i]), _gate_lane_vector(dt_bias[i]))
        o_gdn = _gated_deltanet(z, conv_w[i], gcum, gdn_norm[i].reshape(1, GDN_DV), batch=batch, lc=lc)
        o_attn = _windowed_attention(z, attn_sink[i], cos, sin, batch=batch, lc=lc)
        xs = _out_projection(xs, o_attn, o_gdn, w_out[i].astype(BF16), mod[i], batch=batch, lc=lc)
        xs = _ffn(xs, norm_ffn[i].reshape(1, D_MODEL), mod[i], w_ff1[i].astype(BF16),
                  w_ff2[i].astype(BF16), gfin, batch=batch, lc=lc, final_norm=(i == depth - 1))
    return xs.reshape(batch, t, D_MODEL)[:, lc:, :]
```
